```python
import math
import jax, jax.numpy as jnp
from jax import lax
import numpy as np

D_MODEL = 1024
BATCH = 2
SEQ = 16384
DEPTH = 2

N_A_LAYERS = DEPTH // 2
N_B_LAYERS = DEPTH - N_A_LAYERS
CONV_WIDTH = 3
N_HEADS = 8
HEAD_DIM = 64
ATT_WIDTH = N_HEADS * HEAD_DIM
Q_BLOCK = 128
RMS_EPS = 1e-6

kernel_name = "yoco_shortconv_stickbreaking"


def _rmsnorm(x, g):
    xf = x.astype(jnp.float32)
    y = xf * lax.rsqrt(jnp.mean(xf * xf, axis=-1, keepdims=True) + RMS_EPS)
    return (y * g.astype(jnp.float32)).astype(x.dtype)


def _causal_depthwise_conv(u, w):
    return lax.conv_general_dilated(
        u, w[:, None, :].astype(u.dtype),
        window_strides=(1,),
        padding=[(CONV_WIDTH - 1, 0)],
        dimension_numbers=("NWC", "WIO", "NWC"),
        feature_group_count=u.shape[-1])


def _short_conv_mixer(h, w_in, w_conv, w_out):
    u = h @ w_in
    b_gate, c_gate, xin, g = jnp.split(u, 4, axis=-1)
    y = b_gate * _causal_depthwise_conv(c_gate * xin, w_conv)
    return (y * jax.nn.silu(g)) @ w_out


def _stick_breaking_attention(q, k, v):
    seq = q.shape[2]
    n_blocks = seq // Q_BLOCK
    scale = HEAD_DIM ** -0.5
    qf = q.astype(jnp.float32)
    kf = k.astype(jnp.float32)
    vf = v.astype(jnp.float32)
    outs = []
    for blk in range(n_blocks):
        q0 = blk * Q_BLOCK
        kl = q0 + Q_BLOCK
        qb = qf[:, :, q0:kl]
        z = jnp.einsum("bhqd,bhkd->bhqk", qb, kf[:, :, :kl]) * scale
        q_pos = q0 + jnp.arange(Q_BLOCK)
        mask = jnp.arange(kl)[None, :] < q_pos[:, None]
        log_one_minus = jnp.where(mask, jax.nn.log_sigmoid(-z), 0.0)
        rev = lax.cumsum(log_one_minus, axis=3, reverse=True)
        weights = jnp.exp(jnp.where(mask, z + rev, -jnp.inf))
        outs.append(jnp.einsum("bhqk,bhkd->bhqd", weights, vf[:, :, :kl]))
    o = jnp.concatenate(outs, axis=2)
    return o.astype(q.dtype)


def setup_inputs(seed: int = 0) -> dict:
    key = jax.random.key(seed)
    ks = jax.random.split(key, 12)
    D = D_MODEL
    s = D ** -0.5
    f32 = jnp.float32
    return {
        "x": jax.random.normal(ks[0], (BATCH, SEQ, D), f32),
        "norm_a": 1.0 + 0.02 * jax.random.normal(ks[1], (N_A_LAYERS, D), f32),
        "w_in_a": s * jax.random.normal(ks[2], (N_A_LAYERS, D, 4 * D), f32),
        "conv_a": (CONV_WIDTH ** -0.5) * jax.random.normal(ks[3], (N_A_LAYERS, CONV_WIDTH, D), f32),
        "w_out_a": s * jax.random.normal(ks[4], (N_A_LAYERS, D, D), f32),
        "norm_kv": 1.0 + 0.02 * jax.random.normal(ks[5], (D,), f32),
        "w_kv": s * jax.random.normal(ks[6], (D, 2 * ATT_WIDTH), f32),
        "norm_b": 1.0 + 0.02 * jax.random.normal(ks[7], (N_B_LAYERS, D), f32),
        "w_in_b": s * jax.random.normal(ks[8], (N_B_LAYERS, D, 2 * ATT_WIDTH), f32),
        "w_out_b": (ATT_WIDTH ** -0.5) * jax.random.normal(ks[9], (N_B_LAYERS, ATT_WIDTH, D), f32),
        "norm_f": 1.0 + 0.02 * jax.random.normal(ks[10], (D,), f32),
    }


def reference(x, norm_a, w_in_a, conv_a, w_out_a, norm_kv, w_kv, norm_b, w_in_b, w_out_b, norm_f):
    bsz, seq, _ = x.shape
    k = v = None
    for layer in range(DEPTH):
        if layer < N_A_LAYERS:
            i = layer
            h = _rmsnorm(x, norm_a[i])
            x = x + _short_conv_mixer(h, w_in_a[i], conv_a[i], w_out_a[i])
            if layer == N_A_LAYERS - 1:
                kv = _rmsnorm(x, norm_kv) @ w_kv
                k, v = jnp.split(kv, 2, axis=-1)
                k = k.reshape(bsz, seq, N_HEADS, HEAD_DIM).transpose(0, 2, 1, 3)
                v = v.reshape(bsz, seq, N_HEADS, HEAD_DIM).transpose(0, 2, 1, 3)
        else:
            j = layer - N_A_LAYERS
            h = _rmsnorm(x, norm_b[j])
            q, g = jnp.split(h @ w_in_b[j], 2, axis=-1)
            q = q.reshape(bsz, seq, N_HEADS, HEAD_DIM).transpose(0, 2, 1, 3)
            o = _stick_breaking_attention(q, k, v)
            o = o.transpose(0, 2, 1, 3).reshape(bsz, seq, ATT_WIDTH)
            x = x + (o * jax.nn.silu(g)) @ w_out_b[j]
    return _rmsnorm(x, norm_f)
```

```python
import functools

import jax
import jax.numpy as jnp
from jax import lax
from jax.experimental import pallas as pl
from jax.experimental.pallas import tpu as pltpu

D_MODEL = 1024
N_HEADS = 8
HEAD_DIM = 64
ATT_WIDTH = N_HEADS * HEAD_DIM
CONV_WIDTH = 3
RMS_EPS = 1e-6

LANES = 128
SUBLANES = 8
HEADS_PER_LANE_TILE = LANES // HEAD_DIM
N_HEAD_PAIRS = ATT_WIDTH // LANES

ROWS_A = 512
COLS_A = 256
Q_BLOCK = 128
K_BLOCK = 128

LOG_WEIGHT_FLOOR = -106.0

VMEM_LIMIT_A = 48 * 1024 * 1024
VMEM_LIMIT_B = 52 * 1024 * 1024

_F32 = jnp.float32
_BF16 = jnp.bfloat16


def _dot(a, b):
    return jnp.dot(a, b, preferred_element_type=_F32)


def _silu(x):
    return x * (1.0 / (1.0 + jnp.exp(-x)))


def _inv_rms(x):
    return lax.rsqrt(jnp.mean(x * x, axis=-1, keepdims=True) + RMS_EPS)


def _layer_a_kernel(x_ref, norm_a_ref, w_in_ref, conv_ref, w_out_ref, norm_kv_ref, w_kv_ref,
                    norm_b_ref, w_in_b_ref,
                    x1_ref, q_ref, k_ref, v_ref, g_ref,
                    cx_ref):
    rows = x_ref.shape[1]

    @pl.when(pl.program_id(1) == 0)
    def _():
        cx_ref[0:SUBLANES, :] = jnp.zeros((SUBLANES, D_MODEL), _F32)

    x = x_ref[0]
    h = ((x * _inv_rms(x)) * norm_a_ref[...]).astype(_BF16)
    x1_ref[0] = x
    for c in range(D_MODEL // COLS_A):
        lo = c * COLS_A
        cols = slice(lo, lo + COLS_A)
        b_gate = _dot(h, w_in_ref[:, lo:lo + COLS_A])
        c_gate = _dot(h, w_in_ref[:, D_MODEL + lo:D_MODEL + lo + COLS_A])
        x_in = _dot(h, w_in_ref[:, 2 * D_MODEL + lo:2 * D_MODEL + lo + COLS_A])
        gate = _dot(h, w_in_ref[:, 3 * D_MODEL + lo:3 * D_MODEL + lo + COLS_A])
        cx = c_gate * x_in
        cx_ref[SUBLANES:SUBLANES + rows, cols] = cx
        w = conv_ref[:, cols]
        conv = (w[0:1] * cx_ref[SUBLANES - 2:SUBLANES - 2 + rows, cols]
                + w[1:2] * cx_ref[SUBLANES - 1:SUBLANES - 1 + rows, cols]
                + w[2:3] * cx)
        cx_ref[0:SUBLANES, cols] = cx_ref[rows:rows + SUBLANES, cols]
        y = ((b_gate * conv) * _silu(gate)).astype(_BF16)
        x1_ref[0] += _dot(y, w_out_ref[cols, :])

    x1 = x1_ref[0]
    xn = x1 * _inv_rms(x1)
    kv = _dot((xn * norm_kv_ref[...]).astype(_BF16), w_kv_ref[...])
    k_ref[0] = kv[:, :ATT_WIDTH].astype(_BF16)
    v_ref[0] = kv[:, ATT_WIDTH:].astype(_BF16)
    qg = _dot((xn * norm_b_ref[...]).astype(_BF16), w_in_b_ref[...])
    q_ref[0] = (qg[:, :ATT_WIDTH] * (HEAD_DIM ** -0.5)).astype(_BF16)
    g_ref[0] = qg[:, ATT_WIDTH:].astype(_BF16)


def _layer_b_kernel(q_ref, g_ref, x1_ref, k_ref, v_ref, tri_ones_ref, w_out_ref, norm_f_ref,
                    out_ref,
                    qm_ref, r_ref, acc_ref):
    i = pl.program_id(1)
    lane = lax.broadcasted_iota(jnp.int32, (Q_BLOCK, LANES), 1)
    row = lax.broadcasted_iota(jnp.int32, (Q_BLOCK, K_BLOCK), 0)
    first_head_lanes = lane < HEAD_DIM
    causal = lane < row

    for p in range(N_HEAD_PAIRS):
        qp = q_ref[0, :, p * LANES:(p + 1) * LANES]
        zero = jnp.zeros_like(qp)
        qm_ref[2 * p] = jnp.where(first_head_lanes, qp, zero)
        qm_ref[2 * p + 1] = jnp.where(first_head_lanes, zero, qp)
    r_ref[...] = jnp.zeros(r_ref.shape, _F32)
    acc_ref[...] = jnp.zeros(acc_ref.shape, _F32)

    def block(j, diagonal):
        start = pl.multiple_of(j * K_BLOCK, K_BLOCK)
        r_max = None
        for p in range(N_HEAD_PAIRS):
            kb = k_ref[0, pl.ds(start, K_BLOCK), p * LANES:(p + 1) * LANES]
            vb = v_ref[0, pl.ds(start, K_BLOCK), p * LANES:(p + 1) * LANES]
            for a in range(HEADS_PER_LANE_TILE):
                hd = HEADS_PER_LANE_TILE * p + a
                z = lax.dot_general(qm_ref[hd], kb, (((1,), (1,)), ((), ())),
                                    preferred_element_type=_F32)
                lom = -(jnp.maximum(z, 0.0) + jnp.log(1.0 + jnp.exp(-jnp.abs(z))))
                if diagonal:
                    lom = jnp.where(causal, lom, 0.0)
                hi = lom.astype(_BF16)
                lo = (lom - hi.astype(_F32)).astype(_BF16)
                cs = _dot(hi, tri_ones_ref[...]) + _dot(lo, tri_ones_ref[...])
                r_prev = r_ref[hd]
                w = jnp.exp(z + cs[:, :K_BLOCK] + r_prev)
                if diagonal:
                    w = jnp.where(causal, w, 0.0)
                acc_ref[hd] += _dot(w.astype(_BF16), vb)
                r_new = r_prev + cs[:, K_BLOCK:]
                r_ref[hd] = r_new
                r_max = r_new if r_max is None else jnp.maximum(r_max, r_new)
        return jnp.max(r_max)

    r_max0 = block(i, True)

    def cond(carry):
        j, r_max = carry
        return jnp.logical_and(j >= 0, r_max >= LOG_WEIGHT_FLOOR)

    def body(carry):
        j, _ = carry
        return j - 1, block(j, False)

    lax.while_loop(cond, body, (i - 1, r_max0))

    o = jnp.concatenate(
        [jnp.where(first_head_lanes, acc_ref[2 * p], acc_ref[2 * p + 1]) for p in range(N_HEAD_PAIRS)],
        axis=1)
    y = (o * _silu(g_ref[0].astype(_F32))).astype(_BF16)
    x2 = x1_ref[0] + _dot(y, w_out_ref[...])
    out_ref[0] = (x2 * _inv_rms(x2)) * norm_f_ref[...]


def _resident(shape):
    return pl.BlockSpec(shape, lambda b, t: (0,) * len(shape), pipeline_mode=pl.Buffered(1))


def _layer_a(x, norm_a, w_in, conv, w_out, norm_kv, w_kv, norm_b, w_in_b):
    bsz, seq, d = x.shape
    assert d == D_MODEL and seq % ROWS_A == 0
    tile = lambda width: pl.BlockSpec((1, ROWS_A, width), lambda b, t: (b, t, 0))
    act = lambda dtype, width: jax.ShapeDtypeStruct((bsz, seq, width), dtype)
    return pl.pallas_call(
        _layer_a_kernel,
        grid=(bsz, seq // ROWS_A),
        in_specs=[
            tile(D_MODEL),
            _resident((1, D_MODEL)),
            _resident((D_MODEL, 4 * D_MODEL)),
            _resident((CONV_WIDTH, D_MODEL)),
            _resident((D_MODEL, D_MODEL)),
            _resident((1, D_MODEL)),
            _resident((D_MODEL, 2 * ATT_WIDTH)),
            _resident((1, D_MODEL)),
            _resident((D_MODEL, 2 * ATT_WIDTH)),
        ],
        out_specs=[tile(D_MODEL), tile(ATT_WIDTH), tile(ATT_WIDTH), tile(ATT_WIDTH), tile(ATT_WIDTH)],
        out_shape=[act(_F32, D_MODEL), act(_BF16, ATT_WIDTH), act(_BF16, ATT_WIDTH),
                   act(_BF16, ATT_WIDTH), act(_BF16, ATT_WIDTH)],
        scratch_shapes=[pltpu.VMEM((ROWS_A + SUBLANES, D_MODEL), _F32)],
        compiler_params=pltpu.CompilerParams(
            dimension_semantics=("arbitrary", "arbitrary"),
            vmem_limit_bytes=VMEM_LIMIT_A),
        name="layer_a",
    )(x, norm_a, w_in, conv, w_out, norm_kv, w_kv, norm_b, w_in_b)


def _layer_b(q, g, x1, k, v, w_out, norm_f):
    bsz, seq, _ = x1.shape
    assert seq % Q_BLOCK == 0 and Q_BLOCK == K_BLOCK
    key = lax.broadcasted_iota(jnp.int32, (K_BLOCK, 2 * K_BLOCK), 0)
    col = lax.broadcasted_iota(jnp.int32, (K_BLOCK, 2 * K_BLOCK), 1)
    tri_ones = jnp.logical_or(key >= col, col >= K_BLOCK).astype(_BF16)
    tile = lambda width: pl.BlockSpec((1, Q_BLOCK, width), lambda b, t: (b, t, 0))
    whole_seq = pl.BlockSpec((1, seq, ATT_WIDTH), lambda b, t: (b, 0, 0), pipeline_mode=pl.Buffered(1))
    return pl.pallas_call(
        _layer_b_kernel,
        grid=(bsz, seq // Q_BLOCK),
        in_specs=[
            tile(ATT_WIDTH),
            tile(ATT_WIDTH),
            tile(D_MODEL),
            whole_seq,
            whole_seq,
            _resident((K_BLOCK, 2 * K_BLOCK)),
            _resident((ATT_WIDTH, D_MODEL)),
            _resident((1, D_MODEL)),
        ],
        out_specs=tile(D_MODEL),
        out_shape=jax.ShapeDtypeStruct((bsz, seq, D_MODEL), _F32),
        scratch_shapes=[
            pltpu.VMEM((N_HEADS, Q_BLOCK, LANES), _BF16),
            pltpu.VMEM((N_HEADS, Q_BLOCK, LANES), _F32),
            pltpu.VMEM((N_HEADS, Q_BLOCK, LANES), _F32),
        ],
        compiler_params=pltpu.CompilerParams(
            dimension_semantics=("arbitrary", "arbitrary"),
            vmem_limit_bytes=VMEM_LIMIT_B),
        name="layer_b",
    )(q, g, x1, k, v, tri_ones, w_out, norm_f)


@jax.jit
def kernel(x, norm_a, w_in_a, conv_a, w_out_a, norm_kv, w_kv, norm_b, w_in_b, w_out_b, norm_f):
    assert norm_a.shape[0] == 1 and norm_b.shape[0] == 1
    row = lambda g: g.reshape(1, D_MODEL)
    x1, q, k, v, g = _layer_a(
        x, row(norm_a[0]), w_in_a[0].astype(_BF16), conv_a[0], w_out_a[0].astype(_BF16),
        row(norm_kv), w_kv.astype(_BF16), row(norm_b[0]), w_in_b[0].astype(_BF16))
    return _layer_b(q, g, x1, k, v, w_out_b[0].astype(_BF16), row(norm_f))
```

```python
import jax
import jax.numpy as jnp
from jax import lax
from jax.experimental import pallas as pl
from jax.experimental.pallas import tpu as pltpu

D_MODEL = 1024
N_HEADS = 8
HEAD_DIM = 64
ATT_WIDTH = N_HEADS * HEAD_DIM
CONV_WIDTH = 3
RMS_EPS = 1e-6

LANES = 128
SUBLANES = 8
HEADS_PER_LANE_TILE = LANES // HEAD_DIM
N_HEAD_PAIRS = ATT_WIDTH // LANES

ROWS_A = 512
COLS_A = 256
Q_BLOCK = 128
N_SUB = 2
ROWS_B = N_SUB * Q_BLOCK
K_BLOCK = 128

ZERO_WEIGHT_DECAY = 106.0

VMEM_LIMIT_A = 48 * 1024 * 1024
VMEM_LIMIT_B = 52 * 1024 * 1024

_F32 = jnp.float32
_BF16 = jnp.bfloat16


def _dot(a, b):
    return jnp.dot(a, b, preferred_element_type=_F32)


def _silu(x):
    return x * (1.0 / (1.0 + jnp.exp(-x)))


def _inv_rms(x):
    return lax.rsqrt(jnp.mean(x * x, axis=-1, keepdims=True) + RMS_EPS)


def _layer_a_kernel(x_ref, norm_a_ref, w_in_ref, conv_ref, w_out_ref, norm_kv_ref, w_kv_ref,
                    norm_b_ref, w_in_b_ref,
                    x1_ref, q_ref, k_ref, v_ref, g_ref,
                    cx_ref):
    rows = x_ref.shape[1]

    @pl.when(pl.program_id(1) == 0)
    def _():
        cx_ref[0:SUBLANES, :] = jnp.zeros((SUBLANES, D_MODEL), _F32)

    x = x_ref[0]
    h = ((x * _inv_rms(x)) * norm_a_ref[...]).astype(_BF16)
    x1_ref[0] = x

    def in_proj(c):
        return [_dot(h, w_in_ref[:, part * D_MODEL + c * COLS_A:part * D_MODEL + (c + 1) * COLS_A])
                for part in range(4)]

    n_chunks = D_MODEL // COLS_A
    nxt = in_proj(0)
    for c in range(n_chunks):
        lo = c * COLS_A
        cols = slice(lo, lo + COLS_A)
        b_gate, c_gate, x_in, gate = nxt
        if c + 1 < n_chunks:
            nxt = in_proj(c + 1)
        cx = c_gate * x_in
        cx_ref[SUBLANES:SUBLANES + rows, cols] = cx
        w = conv_ref[:, cols]
        conv = (w[0:1] * cx_ref[SUBLANES - 2:SUBLANES - 2 + rows, cols]
                + w[1:2] * cx_ref[SUBLANES - 1:SUBLANES - 1 + rows, cols]
                + w[2:3] * cx)
        cx_ref[0:SUBLANES, cols] = cx_ref[rows:rows + SUBLANES, cols]
        y = ((b_gate * conv) * _silu(gate)).astype(_BF16)
        x1_ref[0] += _dot(y, w_out_ref[cols, :])

    x1 = x1_ref[0]
    xn = x1 * _inv_rms(x1)
    kv = _dot((xn * norm_kv_ref[...]).astype(_BF16), w_kv_ref[...])
    k_ref[0] = kv[:, :ATT_WIDTH].astype(_BF16)
    v_ref[0] = kv[:, ATT_WIDTH:].astype(_BF16)
    qg = _dot((xn * norm_b_ref[...]).astype(_BF16), w_in_b_ref[...])
    q_ref[0] = (qg[:, :ATT_WIDTH] * (HEAD_DIM ** -0.5)).astype(_BF16)
    g_ref[0] = qg[:, ATT_WIDTH:].astype(_BF16)


def _layer_b_kernel(q_ref, g_ref, x1_ref, k_ref, v_ref, tri_ones_ref, w_out_ref, norm_f_ref,
                    out_ref,
                    qm_ref, decay_ref, acc_ref):
    i = pl.program_id(1)
    lane = lax.broadcasted_iota(jnp.int32, (Q_BLOCK, LANES), 1)
    row = lax.broadcasted_iota(jnp.int32, (Q_BLOCK, K_BLOCK), 0)
    first_head_lanes = lane < HEAD_DIM
    causal = lane < row

    for s in range(N_SUB):
        for p in range(N_HEAD_PAIRS):
            qp = q_ref[0, s * Q_BLOCK:(s + 1) * Q_BLOCK, p * LANES:(p + 1) * LANES]
            zero = jnp.zeros_like(qp)
            qm_ref[s * N_HEADS + 2 * p] = jnp.where(first_head_lanes, qp, zero)
            qm_ref[s * N_HEADS + 2 * p + 1] = jnp.where(first_head_lanes, zero, qp)
    decay_ref[...] = jnp.zeros(decay_ref.shape, _F32)
    acc_ref[...] = jnp.zeros(acc_ref.shape, _F32)

    def blocks(items, diagonal):
        chains = []
        for s, j in items:
            start = j * K_BLOCK if isinstance(j, int) else pl.multiple_of(j * K_BLOCK, K_BLOCK)
            for hd in range(N_HEADS):
                p = hd // HEADS_PER_LANE_TILE
                chains.append((s * N_HEADS + hd, start, slice(p * LANES, (p + 1) * LANES)))
        z = [lax.dot_general(qm_ref[slot], k_ref[0, pl.ds(start, K_BLOCK), cols],
                             (((1,), (1,)), ((), ())), preferred_element_type=_F32)
             for slot, start, cols in chains]
        cs = []
        for zc in z:
            sp = jnp.maximum(zc, 0.0) + jnp.log(1.0 + jnp.exp(-jnp.abs(zc)))
            if diagonal:
                sp = jnp.where(causal, sp, 0.0)
            hi = sp.astype(_BF16)
            lo = (sp - hi.astype(_F32)).astype(_BF16)
            cs.append(_dot(jnp.concatenate([hi, lo], axis=1), tri_ones_ref[...]))
        d_min = None
        for (slot, start, cols), zc, csc in zip(chains, z, cs):
            d_prev = decay_ref[slot]
            w = jnp.exp(zc - csc[:, :K_BLOCK] - d_prev)
            if diagonal:
                w = jnp.where(causal, w, 0.0)
            acc_ref[slot] += _dot(w.astype(_BF16), v_ref[0, pl.ds(start, K_BLOCK), cols])
            d_new = d_prev + csc[:, K_BLOCK:]
            decay_ref[slot] = d_new
            d_min = d_new if d_min is None else jnp.minimum(d_min, d_new)
        return jnp.min(d_min)

    first = N_SUB * i
    d_min0 = blocks([(s, first + s) for s in range(N_SUB)], True)

    def cond(carry):
        n, d_min = carry
        return jnp.logical_and(n <= first, d_min <= ZERO_WEIGHT_DECAY)

    def body(carry):
        n, _ = carry
        return n + 1, blocks([(s, first + s - n) for s in range(N_SUB)], False)

    n_end, d_min_end = lax.while_loop(cond, body, (jnp.int32(1), d_min0))

    @pl.when(jnp.logical_and(n_end > first, d_min_end <= ZERO_WEIGHT_DECAY))
    def _():
        for t in range(1, N_SUB):
            blocks([(s, s - t) for s in range(t, N_SUB)], False)

    o = jnp.concatenate(
        [jnp.concatenate(
            [jnp.where(first_head_lanes, acc_ref[s * N_HEADS + 2 * p], acc_ref[s * N_HEADS + 2 * p + 1])
             for p in range(N_HEAD_PAIRS)], axis=1)
         for s in range(N_SUB)], axis=0)
    y = (o * _silu(g_ref[0].astype(_F32))).astype(_BF16)
    x2 = x1_ref[0] + _dot(y, w_out_ref[...])
    out_ref[0] = (x2 * _inv_rms(x2)) * norm_f_ref[...]


def _resident(shape):
    return pl.BlockSpec(shape, lambda b, t: (0,) * len(shape), pipeline_mode=pl.Buffered(1))


def _layer_a(x, norm_a, w_in, conv, w_out, norm_kv, w_kv, norm_b, w_in_b):
    bsz, seq, d = x.shape
    assert d == D_MODEL and seq % ROWS_A == 0
    tile = lambda width: pl.BlockSpec((1, ROWS_A, width), lambda b, t: (b, t, 0))
    act = lambda dtype, width: jax.ShapeDtypeStruct((bsz, seq, width), dtype)
    return pl.pallas_call(
        _layer_a_kernel,
        grid=(bsz, seq // ROWS_A),
        in_specs=[
            tile(D_MODEL),
            _resident((1, D_MODEL)),
            _resident((D_MODEL, 4 * D_MODEL)),
            _resident((CONV_WIDTH, D_MODEL)),
            _resident((D_MODEL, D_MODEL)),
            _resident((1, D_MODEL)),
            _resident((D_MODEL, 2 * ATT_WIDTH)),
            _resident((1, D_MODEL)),
            _resident((D_MODEL, 2 * ATT_WIDTH)),
        ],
        out_specs=[tile(D_MODEL), tile(ATT_WIDTH), tile(ATT_WIDTH), tile(ATT_WIDTH), tile(ATT_WIDTH)],
        out_shape=[act(_F32, D_MODEL), act(_BF16, ATT_WIDTH), act(_BF16, ATT_WIDTH),
                   act(_BF16, ATT_WIDTH), act(_BF16, ATT_WIDTH)],
        scratch_shapes=[pltpu.VMEM((ROWS_A + SUBLANES, D_MODEL), _F32)],
        compiler_params=pltpu.CompilerParams(
            dimension_semantics=("arbitrary", "arbitrary"),
            vmem_limit_bytes=VMEM_LIMIT_A),
        name="layer_a",
    )(x, norm_a, w_in, conv, w_out, norm_kv, w_kv, norm_b, w_in_b)


def _layer_b(q, g, x1, k, v, w_out, norm_f):
    bsz, seq, _ = x1.shape
    assert seq % ROWS_B == 0 and Q_BLOCK == K_BLOCK
    key = lax.broadcasted_iota(jnp.int32, (2 * K_BLOCK, 2 * K_BLOCK), 0) % K_BLOCK
    col = lax.broadcasted_iota(jnp.int32, (2 * K_BLOCK, 2 * K_BLOCK), 1)
    tri_ones = jnp.logical_or(key >= col, col >= K_BLOCK).astype(_BF16)
    tile = lambda width: pl.BlockSpec((1, ROWS_B, width), lambda b, t: (b, t, 0))
    whole_seq = pl.BlockSpec((1, seq, ATT_WIDTH), lambda b, t: (b, 0, 0), pipeline_mode=pl.Buffered(1))
    return pl.pallas_call(
        _layer_b_kernel,
        grid=(bsz, seq // ROWS_B),
        in_specs=[
            tile(ATT_WIDTH),
            tile(ATT_WIDTH),
            tile(D_MODEL),
            whole_seq,
            whole_seq,
            _resident((2 * K_BLOCK, 2 * K_BLOCK)),
            _resident((ATT_WIDTH, D_MODEL)),
            _resident((1, D_MODEL)),
        ],
        out_specs=tile(D_MODEL),
        out_shape=jax.ShapeDtypeStruct((bsz, seq, D_MODEL), _F32),
        scratch_shapes=[
            pltpu.VMEM((N_SUB * N_HEADS, Q_BLOCK, LANES), _BF16),
            pltpu.VMEM((N_SUB * N_HEADS, Q_BLOCK, LANES), _F32),
            pltpu.VMEM((N_SUB * N_HEADS, Q_BLOCK, LANES), _F32),
        ],
        compiler_params=pltpu.CompilerParams(
            dimension_semantics=("arbitrary", "arbitrary"),
            vmem_limit_bytes=VMEM_LIMIT_B),
        name="layer_b",
    )(q, g, x1, k, v, tri_ones, w_out, norm_f)


@jax.jit
def kernel(x, norm_a, w_in_a, conv_a, w_out_a, norm_kv, w_kv, norm_b, w_in_b, w_out_b, norm_f):
    assert norm_a.shape[0] == 1 and norm_b.shape[0] == 1
    row = lambda g: g.reshape(1, D_MODEL)
    x1, q, k, v, g = _layer_a(
        x, row(norm_a[0]), w_in_a[0].astype(_BF16), conv_a[0], w_out_a[0].astype(_BF16),
        row(norm_kv), w_kv.astype(_BF16), row(norm_b[0]), w_in_b[0].astype(_BF16))
    return _layer_b(q, g, x1, k, v, w_out_b[0].astype(_BF16), row(norm_f))
```

```python
import jax
import jax.numpy as jnp
from jax import lax
from jax.experimental import pallas as pl
from jax.experimental.pallas import tpu as pltpu

D_MODEL = 1024
N_HEADS = 8
HEAD_DIM = 64
ATT_WIDTH = N_HEADS * HEAD_DIM
CONV_WIDTH = 3
RMS_EPS = 1e-6

LANES = 128
SUBLANES = 8
HEADS_PER_LANE_TILE = LANES // HEAD_DIM
N_HEAD_PAIRS = ATT_WIDTH // LANES

ROWS_A = 512
COLS_A = 256
Q_BLOCK = 64
N_SUB = 4
FIRST_WINDOWS = 2
ROWS_B = N_SUB * Q_BLOCK
K_BLOCK = 128

ZERO_WEIGHT_DECAY = 106.0

VMEM_LIMIT_A = 48 * 1024 * 1024
VMEM_LIMIT_B = 52 * 1024 * 1024

_F32 = jnp.float32
_BF16 = jnp.bfloat16


def _dot(a, b):
    return jnp.dot(a, b, preferred_element_type=_F32)


def _silu(x):
    return x * (1.0 / (1.0 + jnp.exp(-x)))


def _inv_rms(x):
    return lax.rsqrt(jnp.mean(x * x, axis=-1, keepdims=True) + RMS_EPS)


def _layer_a_kernel(x_ref, norm_a_ref, w_in_ref, conv_ref, w_out_ref, norm_kv_ref, w_kv_ref,
                    norm_b_ref, w_in_b_ref,
                    x1_ref, q_ref, k_ref, v_ref, g_ref,
                    cx_ref):
    rows = x_ref.shape[1]

    @pl.when(pl.program_id(1) == 0)
    def _():
        cx_ref[0:SUBLANES, :] = jnp.zeros((SUBLANES, D_MODEL), _F32)

    x = x_ref[0]
    h = ((x * _inv_rms(x)) * norm_a_ref[...]).astype(_BF16)
    x1_ref[0] = x

    def in_proj(c):
        return [_dot(h, w_in_ref[:, part * D_MODEL + c * COLS_A:part * D_MODEL + (c + 1) * COLS_A])
                for part in range(4)]

    n_chunks = D_MODEL // COLS_A
    nxt = in_proj(0)
    for c in range(n_chunks):
        lo = c * COLS_A
        cols = slice(lo, lo + COLS_A)
        b_gate, c_gate, x_in, gate = nxt
        if c + 1 < n_chunks:
            nxt = in_proj(c + 1)
        cx = c_gate * x_in
        cx_ref[SUBLANES:SUBLANES + rows, cols] = cx
        w = conv_ref[:, cols]
        conv = (w[0:1] * cx_ref[SUBLANES - 2:SUBLANES - 2 + rows, cols]
                + w[1:2] * cx_ref[SUBLANES - 1:SUBLANES - 1 + rows, cols]
                + w[2:3] * cx)
        cx_ref[0:SUBLANES, cols] = cx_ref[rows:rows + SUBLANES, cols]
        y = ((b_gate * conv) * _silu(gate)).astype(_BF16)
        x1_ref[0] += _dot(y, w_out_ref[cols, :])

    x1 = x1_ref[0]
    xn = x1 * _inv_rms(x1)
    kv = _dot((xn * norm_kv_ref[...]).astype(_BF16), w_kv_ref[...])
    k_ref[0] = kv[:, :ATT_WIDTH].astype(_BF16)
    v_ref[0] = kv[:, ATT_WIDTH:].astype(_BF16)
    qg = _dot((xn * norm_b_ref[...]).astype(_BF16), w_in_b_ref[...])
    q_ref[0] = (qg[:, :ATT_WIDTH] * (HEAD_DIM ** -0.5)).astype(_BF16)
    g_ref[0] = qg[:, ATT_WIDTH:].astype(_BF16)


def _layer_b_kernel(q_ref, g_ref, x1_ref, k_ref, v_ref, tri_ones_ref, w_out_ref, norm_f_ref,
                    out_ref,
                    qm_ref, decay_ref, acc_ref):
    i = pl.program_id(1)
    pair_rows = HEADS_PER_LANE_TILE * Q_BLOCK
    lane = lax.broadcasted_iota(jnp.int32, (pair_rows, LANES), 1)
    query = lax.broadcasted_iota(jnp.int32, (pair_rows, K_BLOCK), 0) & (Q_BLOCK - 1)
    first_head_lanes = lax.broadcasted_iota(jnp.int32, (Q_BLOCK, LANES), 1) < HEAD_DIM

    for s in range(N_SUB):
        for p in range(N_HEAD_PAIRS):
            qp = q_ref[0, s * Q_BLOCK:(s + 1) * Q_BLOCK, p * LANES:(p + 1) * LANES]
            zero = jnp.zeros_like(qp)
            qm_ref[s * N_HEAD_PAIRS + p] = jnp.concatenate(
                [jnp.where(first_head_lanes, qp, zero), jnp.where(first_head_lanes, zero, qp)], axis=0)
    decay_ref[...] = jnp.zeros(decay_ref.shape, _F32)
    acc_ref[...] = jnp.zeros(acc_ref.shape, _F32)

    def windows(items):
        pairs = range(N_HEAD_PAIRS)
        cols = lambda p: slice(p * LANES, (p + 1) * LANES)
        masks, starts = [], []
        for s, start, causal_shift, limit in items:
            valid = None
            if causal_shift is not None:
                valid = lane + causal_shift < query
            if limit is not None:
                below = lane + start < limit
                valid = below if valid is None else jnp.logical_and(valid, below)
            masks.append(valid)
            starts.append(start if isinstance(start, int) else pl.multiple_of(start, Q_BLOCK))
        z = [[lax.dot_general(qm_ref[s * N_HEAD_PAIRS + p], k_ref[0, pl.ds(start, K_BLOCK), cols(p)],
                              (((1,), (1,)), ((), ())), preferred_element_type=_F32)
              for p in pairs]
             for (s, _, _, _), start in zip(items, starts)]
        cs = []
        for zw, valid in zip(z, masks):
            terms = []
            for zc in zw:
                sp = jnp.maximum(zc, 0.0) + jnp.log(1.0 + jnp.exp(-jnp.abs(zc)))
                if valid is not None:
                    sp = jnp.where(valid, sp, 0.0)
                hi = sp.astype(_BF16)
                lo = (sp - hi.astype(_F32)).astype(_BF16)
                terms.append(jnp.concatenate([hi, lo], axis=1))
            cs.append(_dot(jnp.concatenate(terms, axis=0), tri_ones_ref[...]))
        for (s, _, _, _), start, valid, zw, csw in zip(items, starts, masks, z, cs):
            for p in pairs:
                slot = s * N_HEAD_PAIRS + p
                csc = csw[p * pair_rows:(p + 1) * pair_rows]
                d_prev = decay_ref[slot]
                w = jnp.exp(zw[p] - csc[:, :K_BLOCK] - d_prev)
                if valid is not None:
                    w = jnp.where(valid, w, 0.0)
                acc_ref[slot] += _dot(w.astype(_BF16), v_ref[0, pl.ds(start, K_BLOCK), cols(p)])
                decay_ref[slot] = d_prev + csc[:, K_BLOCK:]
        d_min = decay_ref[0]
        for slot in range(1, N_SUB * N_HEAD_PAIRS):
            d_min = jnp.minimum(d_min, decay_ref[slot])
        return jnp.min(d_min)

    def first_section(step):
        items = []
        for s in range(N_SUB):
            q0 = ROWS_B * step + Q_BLOCK * s
            for wdw in range(FIRST_WINDOWS):
                shift = Q_BLOCK - (wdw + 1) * K_BLOCK
                start, limit = q0 + shift, None
                if isinstance(step, int):
                    if start + K_BLOCK <= 0:
                        continue
                    if start < 0:
                        start, limit = 0, start + K_BLOCK
                    shift = start - q0
                items.append((s, start, shift if shift + K_BLOCK > 0 else None, limit))
        return windows(items)

    d_min0 = lax.cond(i > 0, lambda: first_section(i), lambda: first_section(0))

    base = [ROWS_B * i + Q_BLOCK * (s + 1) - FIRST_WINDOWS * K_BLOCK for s in range(N_SUB)]

    def cond(carry):
        n, d_min = carry
        return jnp.logical_and(base[N_SUB - 1] - n * K_BLOCK > 0, d_min <= ZERO_WEIGHT_DECAY)

    def body(carry):
        n, _ = carry
        items = []
        for s in range(N_SUB):
            limit = base[s] - n * K_BLOCK
            items.append((s, jnp.maximum(limit - K_BLOCK, 0), None, limit))
        return n + 1, windows(items)

    lax.while_loop(cond, body, (jnp.int32(0), d_min0))

    o = jnp.concatenate(
        [jnp.concatenate(
            [jnp.where(first_head_lanes, acc_ref[s * N_HEAD_PAIRS + p, :Q_BLOCK],
                       acc_ref[s * N_HEAD_PAIRS + p, Q_BLOCK:])
             for p in range(N_HEAD_PAIRS)], axis=1)
         for s in range(N_SUB)], axis=0)
    y = (o * _silu(g_ref[0].astype(_F32))).astype(_BF16)
    x2 = x1_ref[0] + _dot(y, w_out_ref[...])
    out_ref[0] = (x2 * _inv_rms(x2)) * norm_f_ref[...]


def _resident(shape):
    return pl.BlockSpec(shape, lambda b, t: (0,) * len(shape), pipeline_mode=pl.Buffered(1))


def _layer_a(x, norm_a, w_in, conv, w_out, norm_kv, w_kv, norm_b, w_in_b):
    bsz, seq, d = x.shape
    assert d == D_MODEL and seq % ROWS_A == 0
    tile = lambda width: pl.BlockSpec((1, ROWS_A, width), lambda b, t: (b, t, 0))
    act = lambda dtype, width: jax.ShapeDtypeStruct((bsz, seq, width), dtype)
    return pl.pallas_call(
        _layer_a_kernel,
        grid=(bsz, seq // ROWS_A),
        in_specs=[
            tile(D_MODEL),
            _resident((1, D_MODEL)),
            _resident((D_MODEL, 4 * D_MODEL)),
            _resident((CONV_WIDTH, D_MODEL)),
            _resident((D_MODEL, D_MODEL)),
            _resident((1, D_MODEL)),
            _resident((D_MODEL, 2 * ATT_WIDTH)),
            _resident((1, D_MODEL)),
            _resident((D_MODEL, 2 * ATT_WIDTH)),
        ],
        out_specs=[tile(D_MODEL), tile(ATT_WIDTH), tile(ATT_WIDTH), tile(ATT_WIDTH), tile(ATT_WIDTH)],
        out_shape=[act(_F32, D_MODEL), act(_BF16, ATT_WIDTH), act(_BF16, ATT_WIDTH),
                   act(_BF16, ATT_WIDTH), act(_BF16, ATT_WIDTH)],
        scratch_shapes=[pltpu.VMEM((ROWS_A + SUBLANES, D_MODEL), _F32)],
        compiler_params=pltpu.CompilerParams(
            dimension_semantics=("arbitrary", "arbitrary"),
            vmem_limit_bytes=VMEM_LIMIT_A),
        name="layer_a",
    )(x, norm_a, w_in, conv, w_out, norm_kv, w_kv, norm_b, w_in_b)


def _layer_b(q, g, x1, k, v, w_out, norm_f):
    bsz, seq, _ = x1.shape
    assert seq % ROWS_B == 0 and seq >= FIRST_WINDOWS * K_BLOCK
    key = lax.broadcasted_iota(jnp.int32, (2 * K_BLOCK, 2 * K_BLOCK), 0) % K_BLOCK
    col = lax.broadcasted_iota(jnp.int32, (2 * K_BLOCK, 2 * K_BLOCK), 1)
    tri_ones = jnp.logical_or(key >= col, col >= K_BLOCK).astype(_BF16)
    tile = lambda width: pl.BlockSpec((1, ROWS_B, width), lambda b, t: (b, t, 0))
    whole_seq = pl.BlockSpec((1, seq, ATT_WIDTH), lambda b, t: (b, 0, 0), pipeline_mode=pl.Buffered(1))
    return pl.pallas_call(
        _layer_b_kernel,
        grid=(bsz, seq // ROWS_B),
        in_specs=[
            tile(ATT_WIDTH),
            tile(ATT_WIDTH),
            tile(D_MODEL),
            whole_seq,
            whole_seq,
            _resident((2 * K_BLOCK, 2 * K_BLOCK)),
            _resident((ATT_WIDTH, D_MODEL)),
            _resident((1, D_MODEL)),
        ],
        out_specs=tile(D_MODEL),
        out_shape=jax.ShapeDtypeStruct((bsz, seq, D_MODEL), _F32),
        scratch_shapes=[
            pltpu.VMEM((N_SUB * N_HEAD_PAIRS, HEADS_PER_LANE_TILE * Q_BLOCK, LANES), _BF16),
            pltpu.VMEM((N_SUB * N_HEAD_PAIRS, HEADS_PER_LANE_TILE * Q_BLOCK, LANES), _F32),
            pltpu.VMEM((N_SUB * N_HEAD_PAIRS, HEADS_PER_LANE_TILE * Q_BLOCK, LANES), _F32),
        ],
        compiler_params=pltpu.CompilerParams(
            dimension_semantics=("arbitrary", "arbitrary"),
            vmem_limit_bytes=VMEM_LIMIT_B),
        name="layer_b",
    )(q, g, x1, k, v, tri_ones, w_out, norm_f)


@jax.jit
def kernel(x, norm_a, w_in_a, conv_a, w_out_a, norm_kv, w_kv, norm_b, w_in_b, w_out_b, norm_f):
    assert norm_a.shape[0] == 1 and norm_b.shape[0] == 1
    row = lambda g: g.reshape(1, D_MODEL)
    x1, q, k, v, g = _layer_a(
        x, row(norm_a[0]), w_in_a[0].astype(_BF16), conv_a[0], w_out_a[0].astype(_BF16),
        row(norm_kv), w_kv.astype(_BF16), row(norm_b[0]), w_in_b[0].astype(_BF16))
    return _layer_b(q, g, x1, k, v, w_out_b[0].astype(_BF16), row(norm_f))
```

```python
import jax
import jax.numpy as jnp
from jax import lax
from jax.experimental import pallas as pl
from jax.experimental.pallas import tpu as pltpu

D_MODEL = 1024
N_HEADS = 8
HEAD_DIM = 64
ATT_WIDTH = N_HEADS * HEAD_DIM
CONV_WIDTH = 3
RMS_EPS = 1e-6

LANES = 128
SUBLANES = 8
HEADS_PER_LANE_TILE = LANES // HEAD_DIM
N_HEAD_PAIRS = ATT_WIDTH // LANES

ROWS_A = 512
COLS_A = 512
HALVES_A = 2
Q_BLOCK = 64
N_SUB = 4
FIRST_WINDOWS = 2
ROWS_B = N_SUB * Q_BLOCK
K_BLOCK = 128

LOG2_E = 1.4426950408889634
ZERO_WEIGHT_DECAY = 153.0

VMEM_LIMIT_A = 48 * 1024 * 1024
VMEM_LIMIT_B = 52 * 1024 * 1024

_F32 = jnp.float32
_BF16 = jnp.bfloat16


def _dot(a, b):
    return jnp.dot(a, b, preferred_element_type=_F32)


def _silu(x):
    return x * (1.0 / (1.0 + jnp.exp(-x)))


def _inv_rms(x):
    return lax.rsqrt(jnp.mean(x * x, axis=-1, keepdims=True) + RMS_EPS)


def _layer_a_kernel(x_ref, norm_a_ref, w_in_ref, conv_ref, w_out_ref, norm_kv_ref, w_kv_ref,
                    norm_b_ref, w_in_b_ref,
                    x1_ref, q_ref, k_ref, v_ref, g_ref,
                    cx_ref):
    rows = x_ref.shape[1] // HALVES_A
    n_chunks = D_MODEL // COLS_A

    @pl.when(pl.program_id(1) == 0)
    def _():
        cx_ref[0:SUBLANES, :] = jnp.zeros((SUBLANES, D_MODEL), _F32)

    def mixer(r0):
        x = x_ref[0, r0:r0 + rows]
        h = ((x * _inv_rms(x)) * norm_a_ref[...]).astype(_BF16)
        x1_ref[0, r0:r0 + rows] = x

        def in_proj(c):
            return [_dot(h, w_in_ref[:, part * D_MODEL + c * COLS_A:part * D_MODEL + (c + 1) * COLS_A])
                    for part in range(4)]

        nxt = in_proj(0)
        for c in range(n_chunks):
            cols = slice(c * COLS_A, (c + 1) * COLS_A)
            b_gate, c_gate, x_in, gate = nxt
            if c + 1 < n_chunks:
                nxt = in_proj(c + 1)
            cx = c_gate * x_in
            cx_ref[SUBLANES:SUBLANES + rows, cols] = cx
            w = conv_ref[:, cols]
            conv = (w[0:1] * cx_ref[SUBLANES - 2:SUBLANES - 2 + rows, cols]
                    + w[1:2] * cx_ref[SUBLANES - 1:SUBLANES - 1 + rows, cols]
                    + w[2:3] * cx)
            cx_ref[0:SUBLANES, cols] = cx_ref[rows:rows + SUBLANES, cols]
            y = ((b_gate * conv) * _silu(gate)).astype(_BF16)
            x1_ref[0, r0:r0 + rows] += _dot(y, w_out_ref[cols, :])

    def projections(r0):
        x1 = x1_ref[0, r0:r0 + rows]
        xn = x1 * _inv_rms(x1)
        kv = _dot((xn * norm_kv_ref[...]).astype(_BF16), w_kv_ref[...])
        k_ref[0, r0:r0 + rows] = kv[:, :ATT_WIDTH].astype(_BF16)
        v_ref[0, r0:r0 + rows] = kv[:, ATT_WIDTH:].astype(_BF16)
        qg = _dot((xn * norm_b_ref[...]).astype(_BF16), w_in_b_ref[...])
        q_ref[0, r0:r0 + rows] = (qg[:, :ATT_WIDTH] * (HEAD_DIM ** -0.5 * LOG2_E)).astype(_BF16)
        g_ref[0, r0:r0 + rows] = qg[:, ATT_WIDTH:].astype(_BF16)

    for half in range(HALVES_A):
        mixer(half * rows)
    for half in range(HALVES_A):
        projections(half * rows)


def _layer_b_kernel(q_ref, g_ref, x1_ref, k_ref, v_ref, tri_ones_ref, w_out_ref, norm_f_ref,
                    out_ref,
                    qm_ref, decay_ref, acc_ref):
    i = pl.program_id(1)
    pair_rows = HEADS_PER_LANE_TILE * Q_BLOCK
    lane = lax.broadcasted_iota(jnp.int32, (pair_rows, LANES), 1)
    query = lax.broadcasted_iota(jnp.int32, (pair_rows, K_BLOCK), 0) & (Q_BLOCK - 1)
    first_head_lanes = lax.broadcasted_iota(jnp.int32, (Q_BLOCK, LANES), 1) < HEAD_DIM

    for s in range(N_SUB):
        for p in range(N_HEAD_PAIRS):
            qp = q_ref[0, s * Q_BLOCK:(s + 1) * Q_BLOCK, p * LANES:(p + 1) * LANES]
            zero = jnp.zeros_like(qp)
            qm_ref[s * N_HEAD_PAIRS + p] = jnp.concatenate(
                [jnp.where(first_head_lanes, qp, zero), jnp.where(first_head_lanes, zero, qp)], axis=0)

    def windows(items):
        pairs = range(N_HEAD_PAIRS)
        cols = lambda p: slice(p * LANES, (p + 1) * LANES)
        masks, starts = [], []
        for s, start, causal_shift, limit, _ in items:
            valid = None
            if causal_shift is not None:
                valid = lane + causal_shift < query
            if limit is not None:
                below = lane + start < limit
                valid = below if valid is None else jnp.logical_and(valid, below)
            masks.append(valid)
            starts.append(start if isinstance(start, int) else pl.multiple_of(start, Q_BLOCK))
        z = [[lax.dot_general(qm_ref[s * N_HEAD_PAIRS + p], k_ref[0, pl.ds(start, K_BLOCK), cols(p)],
                              (((1,), (1,)), ((), ())), preferred_element_type=_F32)
              for p in pairs]
             for (s, _, _, _, _), start in zip(items, starts)]
        cs = []
        for zw, valid in zip(z, masks):
            terms = []
            for zc in zw:
                sp = jnp.maximum(zc, 0.0) + jnp.log2(1.0 + jnp.exp2(-jnp.abs(zc)))
                if valid is not None:
                    sp = jnp.where(valid, sp, 0.0)
                hi = sp.astype(_BF16)
                lo = (sp - hi.astype(_F32)).astype(_BF16)
                terms.append(jnp.concatenate([hi, lo], axis=1))
            cs.append(_dot(jnp.concatenate(terms, axis=0), tri_ones_ref[...]))
        for (s, _, _, _, fresh), start, valid, zw, csw in zip(items, starts, masks, z, cs):
            for p in pairs:
                slot = s * N_HEAD_PAIRS + p
                csc = csw[p * pair_rows:(p + 1) * pair_rows]
                log_w = zw[p] - csc[:, :K_BLOCK]
                if not fresh:
                    d_prev = decay_ref[slot]
                    log_w = log_w - d_prev
                w = jnp.exp2(log_w)
                if valid is not None:
                    w = jnp.where(valid, w, 0.0)
                pv = _dot(w.astype(_BF16), v_ref[0, pl.ds(start, K_BLOCK), cols(p)])
                if fresh:
                    acc_ref[slot] = pv
                    decay_ref[slot] = csc[:, K_BLOCK:]
                else:
                    acc_ref[slot] += pv
                    decay_ref[slot] = d_prev + csc[:, K_BLOCK:]
        d_min = decay_ref[0]
        for slot in range(1, N_SUB * N_HEAD_PAIRS):
            d_min = jnp.minimum(d_min, decay_ref[slot])
        return jnp.min(d_min)

    def first_section(step):
        items = []
        for s in range(N_SUB):
            q0 = ROWS_B * step + Q_BLOCK * s
            for wdw in range(FIRST_WINDOWS):
                shift = Q_BLOCK - (wdw + 1) * K_BLOCK
                start, limit = q0 + shift, None
                if isinstance(step, int):
                    if start + K_BLOCK <= 0:
                        continue
                    if start < 0:
                        start, limit = 0, start + K_BLOCK
                    shift = start - q0
                items.append((s, start, shift if shift + K_BLOCK > 0 else None, limit, wdw == 0))
        return windows(items)

    d_min0 = lax.cond(i > 0, lambda: first_section(i), lambda: first_section(0))

    base = [ROWS_B * i + Q_BLOCK * (s + 1) - FIRST_WINDOWS * K_BLOCK for s in range(N_SUB)]

    def cond(carry):
        n, d_min = carry
        return jnp.logical_and(base[N_SUB - 1] - n * K_BLOCK > 0, d_min <= ZERO_WEIGHT_DECAY)

    def body(carry):
        n, _ = carry
        items = []
        for s in range(N_SUB):
            limit = base[s] - n * K_BLOCK
            items.append((s, jnp.maximum(limit - K_BLOCK, 0), None, limit, False))
        return n + 1, windows(items)

    lax.while_loop(cond, body, (jnp.int32(0), d_min0))

    o = jnp.concatenate(
        [jnp.concatenate(
            [jnp.where(first_head_lanes, acc_ref[s * N_HEAD_PAIRS + p, :Q_BLOCK],
                       acc_ref[s * N_HEAD_PAIRS + p, Q_BLOCK:])
             for p in range(N_HEAD_PAIRS)], axis=1)
         for s in range(N_SUB)], axis=0)
    y = (o * _silu(g_ref[0].astype(_F32))).astype(_BF16)
    x2 = x1_ref[0] + _dot(y, w_out_ref[...])
    out_ref[0] = (x2 * _inv_rms(x2)) * norm_f_ref[...]


def _resident(shape):
    return pl.BlockSpec(shape, lambda b, t: (0,) * len(shape), pipeline_mode=pl.Buffered(1))


def _layer_a(x, norm_a, w_in, conv, w_out, norm_kv, w_kv, norm_b, w_in_b):
    bsz, seq, d = x.shape
    assert d == D_MODEL and seq % ROWS_A == 0
    tile = lambda width: pl.BlockSpec((1, ROWS_A, width), lambda b, t: (b, t, 0))
    act = lambda dtype, width: jax.ShapeDtypeStruct((bsz, seq, width), dtype)
    return pl.pallas_call(
        _layer_a_kernel,
        grid=(bsz, seq // ROWS_A),
        in_specs=[
            tile(D_MODEL),
            _resident((1, D_MODEL)),
            _resident((D_MODEL, 4 * D_MODEL)),
            _resident((CONV_WIDTH, D_MODEL)),
            _resident((D_MODEL, D_MODEL)),
            _resident((1, D_MODEL)),
            _resident((D_MODEL, 2 * ATT_WIDTH)),
            _resident((1, D_MODEL)),
            _resident((D_MODEL, 2 * ATT_WIDTH)),
        ],
        out_specs=[tile(D_MODEL), tile(ATT_WIDTH), tile(ATT_WIDTH), tile(ATT_WIDTH), tile(ATT_WIDTH)],
        out_shape=[act(_F32, D_MODEL), act(_BF16, ATT_WIDTH), act(_BF16, ATT_WIDTH),
                   act(_BF16, ATT_WIDTH), act(_BF16, ATT_WIDTH)],
        scratch_shapes=[pltpu.VMEM((ROWS_A // HALVES_A + SUBLANES, D_MODEL), _F32)],
        compiler_params=pltpu.CompilerParams(
            dimension_semantics=("arbitrary", "arbitrary"),
            vmem_limit_bytes=VMEM_LIMIT_A),
        name="layer_a",
    )(x, norm_a, w_in, conv, w_out, norm_kv, w_kv, norm_b, w_in_b)


def _layer_b(q, g, x1, k, v, w_out, norm_f):
    bsz, seq, _ = x1.shape
    assert seq % ROWS_B == 0 and seq >= FIRST_WINDOWS * K_BLOCK
    key = lax.broadcasted_iota(jnp.int32, (2 * K_BLOCK, 2 * K_BLOCK), 0) % K_BLOCK
    col = lax.broadcasted_iota(jnp.int32, (2 * K_BLOCK, 2 * K_BLOCK), 1)
    tri_ones = jnp.logical_or(key >= col, col >= K_BLOCK).astype(_BF16)
    tile = lambda width: pl.BlockSpec((1, ROWS_B, width), lambda b, t: (b, t, 0))
    whole_seq = pl.BlockSpec((1, seq, ATT_WIDTH), lambda b, t: (b, 0, 0), pipeline_mode=pl.Buffered(1))
    return pl.pallas_call(
        _layer_b_kernel,
        grid=(bsz, seq // ROWS_B),
        in_specs=[
            tile(ATT_WIDTH),
            tile(ATT_WIDTH),
            tile(D_MODEL),
            whole_seq,
            whole_seq,
            _resident((2 * K_BLOCK, 2 * K_BLOCK)),
            _resident((ATT_WIDTH, D_MODEL)),
            _resident((1, D_MODEL)),
        ],
        out_specs=tile(D_MODEL),
        out_shape=jax.ShapeDtypeStruct((bsz, seq, D_MODEL), _F32),
        scratch_shapes=[
            pltpu.VMEM((N_SUB * N_HEAD_PAIRS, HEADS_PER_LANE_TILE * Q_BLOCK, LANES), _BF16),
            pltpu.VMEM((N_SUB * N_HEAD_PAIRS, HEADS_PER_LANE_TILE * Q_BLOCK, LANES), _F32),
            pltpu.VMEM((N_SUB * N_HEAD_PAIRS, HEADS_PER_LANE_TILE * Q_BLOCK, LANES), _F32),
        ],
        compiler_params=pltpu.CompilerParams(
            dimension_semantics=("arbitrary", "arbitrary"),
            vmem_limit_bytes=VMEM_LIMIT_B),
        name="layer_b",
    )(q, g, x1, k, v, tri_ones, w_out, norm_f)


@jax.jit
def kernel(x, norm_a, w_in_a, conv_a, w_out_a, norm_kv, w_kv, norm_b, w_in_b, w_out_b, norm_f):
    assert norm_a.shape[0] == 1 and norm_b.shape[0] == 1
    row = lambda g: g.reshape(1, D_MODEL)
    x1, q, k, v, g = _layer_a(
        x, row(norm_a[0]), w_in_a[0].astype(_BF16), conv_a[0], w_out_a[0].astype(_BF16),
        row(norm_kv), w_kv.astype(_BF16), row(norm_b[0]), w_in_b[0].astype(_BF16))
    return _layer_b(q, g, x1, k, v, w_out_b[0].astype(_BF16), row(norm_f))
```

```python
import jax
import jax.numpy as jnp
from jax import lax
from jax.experimental import pallas as pl
from jax.experimental.pallas import tpu as pltpu

D_MODEL = 1024
N_HEADS = 8
HEAD_DIM = 64
ATT_WIDTH = N_HEADS * HEAD_DIM
CONV_WIDTH = 3
RMS_EPS = 1e-6

LANES = 128
SUBLANES = 8
HEADS_PER_LANE_TILE = LANES // HEAD_DIM
N_HEAD_PAIRS = ATT_WIDTH // LANES

ROWS_A = 1024
COLS_A = 512
HALVES_A = 4
Q_BLOCK = 64
N_SUB = 8
FIRST_WINDOWS = 2
ROWS_B = N_SUB * Q_BLOCK
K_BLOCK = 128

LOG2_E = 1.4426950408889634
ZERO_WEIGHT_DECAY = 153.0

VMEM_LIMIT_A = 56 * 1024 * 1024
VMEM_LIMIT_B = 58 * 1024 * 1024

_F32 = jnp.float32
_BF16 = jnp.bfloat16


def _dot(a, b):
    return jnp.dot(a, b, preferred_element_type=_F32)


def _silu(x):
    return x * (1.0 / (1.0 + jnp.exp(-x)))


def _inv_rms(x):
    return lax.rsqrt(jnp.mean(x * x, axis=-1, keepdims=True) + RMS_EPS)


def _layer_a_kernel(x_ref, norm_a_ref, w_in_ref, conv_ref, w_out_ref, norm_kv_ref, w_kv_ref,
                    norm_b_ref, w_in_b_ref,
                    x1_ref, q_ref, k_ref, v_ref, g_ref,
                    cx_ref):
    rows = x_ref.shape[1] // HALVES_A
    n_chunks = D_MODEL // COLS_A

    @pl.when(pl.program_id(1) == 0)
    def _():
        cx_ref[0:SUBLANES, :] = jnp.zeros((SUBLANES, D_MODEL), _F32)

    def mixer(r0):
        x = x_ref[0, r0:r0 + rows]
        h = ((x * _inv_rms(x)) * norm_a_ref[...]).astype(_BF16)
        x1_ref[0, r0:r0 + rows] = x

        def in_proj(c):
            return [_dot(h, w_in_ref[:, part * D_MODEL + c * COLS_A:part * D_MODEL + (c + 1) * COLS_A])
                    for part in range(4)]

        nxt = in_proj(0)
        for c in range(n_chunks):
            cols = slice(c * COLS_A, (c + 1) * COLS_A)
            b_gate, c_gate, x_in, gate = nxt
            if c + 1 < n_chunks:
                nxt = in_proj(c + 1)
            cx = c_gate * x_in
            cx_ref[SUBLANES:SUBLANES + rows, cols] = cx
            w = conv_ref[:, cols]
            conv = (w[0:1] * cx_ref[SUBLANES - 2:SUBLANES - 2 + rows, cols]
                    + w[1:2] * cx_ref[SUBLANES - 1:SUBLANES - 1 + rows, cols]
                    + w[2:3] * cx)
            cx_ref[0:SUBLANES, cols] = cx_ref[rows:rows + SUBLANES, cols]
            y = ((b_gate * conv) * _silu(gate)).astype(_BF16)
            x1_ref[0, r0:r0 + rows] += _dot(y, w_out_ref[cols, :])

    def projections(r0):
        x1 = x1_ref[0, r0:r0 + rows]
        xn = x1 * _inv_rms(x1)
        kv = _dot((xn * norm_kv_ref[...]).astype(_BF16), w_kv_ref[...])
        k_ref[0, r0:r0 + rows] = kv[:, :ATT_WIDTH].astype(_BF16)
        v_ref[0, r0:r0 + rows] = kv[:, ATT_WIDTH:].astype(_BF16)
        qg = _dot((xn * norm_b_ref[...]).astype(_BF16), w_in_b_ref[...])
        q_ref[0, r0:r0 + rows] = (qg[:, :ATT_WIDTH] * (HEAD_DIM ** -0.5 * LOG2_E)).astype(_BF16)
        g_ref[0, r0:r0 + rows] = qg[:, ATT_WIDTH:].astype(_BF16)

    for half in range(HALVES_A):
        mixer(half * rows)
    for half in range(HALVES_A):
        projections(half * rows)


def _layer_b_kernel(q_ref, g_ref, x1_ref, k_ref, v_ref, tri_ones_ref, w_out_ref, norm_f_ref,
                    out_ref,
                    qm_ref, decay_ref, acc_ref):
    i = pl.program_id(1)
    pair_rows = HEADS_PER_LANE_TILE * Q_BLOCK
    lane = lax.broadcasted_iota(jnp.int32, (pair_rows, LANES), 1)
    query = lax.broadcasted_iota(jnp.int32, (pair_rows, K_BLOCK), 0) & (Q_BLOCK - 1)
    first_head_lanes = lax.broadcasted_iota(jnp.int32, (Q_BLOCK, LANES), 1) < HEAD_DIM

    for s in range(N_SUB):
        for p in range(N_HEAD_PAIRS):
            qp = q_ref[0, s * Q_BLOCK:(s + 1) * Q_BLOCK, p * LANES:(p + 1) * LANES]
            zero = jnp.zeros_like(qp)
            qm_ref[s * N_HEAD_PAIRS + p] = jnp.concatenate(
                [jnp.where(first_head_lanes, qp, zero), jnp.where(first_head_lanes, zero, qp)], axis=0)

    def windows(items):
        pairs = range(N_HEAD_PAIRS)
        cols = lambda p: slice(p * LANES, (p + 1) * LANES)
        masks, starts = [], []
        for s, start, causal_shift, limit, _ in items:
            valid = None
            if causal_shift is not None:
                valid = lane + causal_shift < query
            if limit is not None:
                below = lane + start < limit
                valid = below if valid is None else jnp.logical_and(valid, below)
            masks.append(valid)
            starts.append(start if isinstance(start, int) else pl.multiple_of(start, Q_BLOCK))
        z = [[lax.dot_general(qm_ref[s * N_HEAD_PAIRS + p], k_ref[0, pl.ds(start, K_BLOCK), cols(p)],
                              (((1,), (1,)), ((), ())), preferred_element_type=_F32)
              for p in pairs]
             for (s, _, _, _, _), start in zip(items, starts)]
        cs = []
        for zw, valid in zip(z, masks):
            terms = []
            for zc in zw:
                sp = jnp.maximum(zc, 0.0) + jnp.log2(1.0 + jnp.exp2(-jnp.abs(zc)))
                if valid is not None:
                    sp = jnp.where(valid, sp, 0.0)
                hi = sp.astype(_BF16)
                lo = (sp - hi.astype(_F32)).astype(_BF16)
                terms.append(jnp.concatenate([hi, lo], axis=1))
            cs.append(_dot(jnp.concatenate(terms, axis=0), tri_ones_ref[...]))
        for (s, _, _, _, fresh), start, valid, zw, csw in zip(items, starts, masks, z, cs):
            for p in pairs:
                slot = s * N_HEAD_PAIRS + p
                csc = csw[p * pair_rows:(p + 1) * pair_rows]
                log_w = zw[p] - csc[:, :K_BLOCK]
                if not fresh:
                    d_prev = decay_ref[slot]
                    log_w = log_w - d_prev
                w = jnp.exp2(log_w)
                if valid is not None:
                    w = jnp.where(valid, w, 0.0)
                pv = _dot(w.astype(_BF16), v_ref[0, pl.ds(start, K_BLOCK), cols(p)])
                if fresh:
                    acc_ref[slot] = pv
                    decay_ref[slot] = csc[:, K_BLOCK:]
                else:
                    acc_ref[slot] += pv
                    decay_ref[slot] = d_prev + csc[:, K_BLOCK:]
        d_min = decay_ref[0]
        for slot in range(1, N_SUB * N_HEAD_PAIRS):
            d_min = jnp.minimum(d_min, decay_ref[slot])
        return jnp.min(d_min)

    def first_section(step):
        items = []
        for s in range(N_SUB):
            q0 = ROWS_B * step + Q_BLOCK * s
            for wdw in range(FIRST_WINDOWS):
                shift = Q_BLOCK - (wdw + 1) * K_BLOCK
                start, limit = q0 + shift, None
                if isinstance(step, int):
                    if start + K_BLOCK <= 0:
                        continue
                    if start < 0:
                        start, limit = 0, start + K_BLOCK
                    shift = start - q0
                items.append((s, start, shift if shift + K_BLOCK > 0 else None, limit, wdw == 0))
        return windows(items)

    d_min0 = lax.cond(i > 0, lambda: first_section(i), lambda: first_section(0))

    base = [ROWS_B * i + Q_BLOCK * (s + 1) - FIRST_WINDOWS * K_BLOCK for s in range(N_SUB)]

    def cond(carry):
        n, d_min = carry
        return jnp.logical_and(base[N_SUB - 1] - n * K_BLOCK > 0, d_min <= ZERO_WEIGHT_DECAY)

    def body(carry):
        n, _ = carry
        items = []
        for s in range(N_SUB):
            limit = base[s] - n * K_BLOCK
            items.append((s, jnp.maximum(limit - K_BLOCK, 0), None, limit, False))
        return n + 1, windows(items)

    lax.while_loop(cond, body, (jnp.int32(0), d_min0))

    o = jnp.concatenate(
        [jnp.concatenate(
            [jnp.where(first_head_lanes, acc_ref[s * N_HEAD_PAIRS + p, :Q_BLOCK],
                       acc_ref[s * N_HEAD_PAIRS + p, Q_BLOCK:])
             for p in range(N_HEAD_PAIRS)], axis=1)
         for s in range(N_SUB)], axis=0)
    y = (o * _silu(g_ref[0].astype(_F32))).astype(_BF16)
    x2 = x1_ref[0] + _dot(y, w_out_ref[...])
    out_ref[0] = (x2 * _inv_rms(x2)) * norm_f_ref[...]


def _resident(shape):
    return pl.BlockSpec(shape, lambda b, t: (0,) * len(shape), pipeline_mode=pl.Buffered(1))


def _layer_a(x, norm_a, w_in, conv, w_out, norm_kv, w_kv, norm_b, w_in_b):
    bsz, seq, d = x.shape
    assert d == D_MODEL and seq % ROWS_A == 0
    tile = lambda width: pl.BlockSpec((1, ROWS_A, width), lambda b, t: (b, t, 0))
    act = lambda dtype, width: jax.ShapeDtypeStruct((bsz, seq, width), dtype)
    return pl.pallas_call(
        _layer_a_kernel,
        grid=(bsz, seq // ROWS_A),
        in_specs=[
            tile(D_MODEL),
            _resident((1, D_MODEL)),
            _resident((D_MODEL, 4 * D_MODEL)),
            _resident((CONV_WIDTH, D_MODEL)),
            _resident((D_MODEL, D_MODEL)),
            _resident((1, D_MODEL)),
            _resident((D_MODEL, 2 * ATT_WIDTH)),
            _resident((1, D_MODEL)),
            _resident((D_MODEL, 2 * ATT_WIDTH)),
        ],
        out_specs=[tile(D_MODEL), tile(ATT_WIDTH), tile(ATT_WIDTH), tile(ATT_WIDTH), tile(ATT_WIDTH)],
        out_shape=[act(_F32, D_MODEL), act(_BF16, ATT_WIDTH), act(_BF16, ATT_WIDTH),
                   act(_BF16, ATT_WIDTH), act(_BF16, ATT_WIDTH)],
        scratch_shapes=[pltpu.VMEM((ROWS_A // HALVES_A + SUBLANES, D_MODEL), _F32)],
        compiler_params=pltpu.CompilerParams(
            dimension_semantics=("arbitrary", "arbitrary"),
            vmem_limit_bytes=VMEM_LIMIT_A),
        name="layer_a",
    )(x, norm_a, w_in, conv, w_out, norm_kv, w_kv, norm_b, w_in_b)


def _layer_b(q, g, x1, k, v, w_out, norm_f):
    bsz, seq, _ = x1.shape
    assert seq % ROWS_B == 0 and seq >= FIRST_WINDOWS * K_BLOCK
    key = lax.broadcasted_iota(jnp.int32, (2 * K_BLOCK, 2 * K_BLOCK), 0) % K_BLOCK
    col = lax.broadcasted_iota(jnp.int32, (2 * K_BLOCK, 2 * K_BLOCK), 1)
    tri_ones = jnp.logical_or(key >= col, col >= K_BLOCK).astype(_BF16)
    tile = lambda width: pl.BlockSpec((1, ROWS_B, width), lambda b, t: (b, t, 0))
    whole_seq = pl.BlockSpec((1, seq, ATT_WIDTH), lambda b, t: (b, 0, 0), pipeline_mode=pl.Buffered(1))
    return pl.pallas_call(
        _layer_b_kernel,
        grid=(bsz, seq // ROWS_B),
        in_specs=[
            tile(ATT_WIDTH),
            tile(ATT_WIDTH),
            tile(D_MODEL),
            whole_seq,
            whole_seq,
            _resident((2 * K_BLOCK, 2 * K_BLOCK)),
            _resident((ATT_WIDTH, D_MODEL)),
            _resident((1, D_MODEL)),
        ],
        out_specs=tile(D_MODEL),
        out_shape=jax.ShapeDtypeStruct((bsz, seq, D_MODEL), _F32),
        scratch_shapes=[
            pltpu.VMEM((N_SUB * N_HEAD_PAIRS, HEADS_PER_LANE_TILE * Q_BLOCK, LANES), _BF16),
            pltpu.VMEM((N_SUB * N_HEAD_PAIRS, HEADS_PER_LANE_TILE * Q_BLOCK, LANES), _F32),
            pltpu.VMEM((N_SUB * N_HEAD_PAIRS, HEADS_PER_LANE_TILE * Q_BLOCK, LANES), _F32),
        ],
        compiler_params=pltpu.CompilerParams(
            dimension_semantics=("arbitrary", "arbitrary"),
            vmem_limit_bytes=VMEM_LIMIT_B),
        name="layer_b",
    )(q, g, x1, k, v, tri_ones, w_out, norm_f)


@jax.jit
def kernel(x, norm_a, w_in_a, conv_a, w_out_a, norm_kv, w_kv, norm_b, w_in_b, w_out_b, norm_f):
    assert norm_a.shape[0] == 1 and norm_b.shape[0] == 1
    row = lambda g: g.reshape(1, D_MODEL)
    x1, q, k, v, g = _layer_a(
        x, row(norm_a[0]), w_in_a[0].astype(_BF16), conv_a[0], w_out_a[0].astype(_BF16),
        row(norm_kv), w_kv.astype(_BF16), row(norm_b[0]), w_in_b[0].astype(_BF16))
    return _layer_b(q, g, x1, k, v, w_out_b[0].astype(_BF16), row(norm_f))
```
